```python
import jax, jax.numpy as jnp
from jax import lax
import numpy as np

D_MODEL = 1024
BATCH = 4
SEQ = 4096
DEPTH = 1

C_CONV = D_MODEL
CONV_WIDTH = 31
POOL_WINDOWS = (2, 4, 8, 16)
N_POOL_GROUPS = len(POOL_WINDOWS)
C_POOL = D_MODEL
POOL_GROUP = C_POOL // N_POOL_GROUPS
N_BRANCHES = 2
D_FF = 4 * D_MODEL
IN_COLS = 2 * C_CONV + C_POOL + N_BRANCHES * D_MODEL
RMS_EPS = 1e-6
LN_EPS = 1e-5

kernel_name = "hybrid_conv_pool_gated_block"


def rms_norm(x, g):
    xf = x.astype(jnp.float32)
    y = xf * lax.rsqrt(jnp.mean(xf * xf, axis=-1, keepdims=True) + RMS_EPS)
    return (y * g.astype(jnp.float32)).astype(x.dtype)


def layer_norm(x, g, b):
    xf = x.astype(jnp.float32)
    mu = jnp.mean(xf, axis=-1, keepdims=True)
    var = jnp.mean(jnp.square(xf - mu), axis=-1, keepdims=True)
    y = (xf - mu) * lax.rsqrt(var + LN_EPS)
    return (y * g.astype(jnp.float32) + b.astype(jnp.float32)).astype(x.dtype)


def depthwise_causal_conv(u, k, b):
    out = lax.conv_general_dilated(
        u, k[:, None, :].astype(u.dtype), window_strides=(1,),
        padding=[(CONV_WIDTH - 1, 0)],
        dimension_numbers=("NWC", "WIO", "NWC"),
        feature_group_count=u.shape[-1])
    return out + b.astype(u.dtype)


def conformer_conv_branch(u_glu, dw_kernel, dw_bias, ln_g, ln_b, w_out):
    a, gate = jnp.split(u_glu, 2, axis=-1)
    u = a * jax.nn.sigmoid(gate)
    u = depthwise_causal_conv(u, dw_kernel, dw_bias)
    u = layer_norm(u, ln_g, ln_b)
    u = jax.nn.swish(u)
    return u @ w_out


def causal_multiscale_pool(p):
    B, S, _ = p.shape
    pg = p.reshape(B, S, N_POOL_GROUPS, POOL_GROUP)
    cs = jnp.cumsum(pg.astype(jnp.float32), axis=1)
    pos = jnp.arange(1, S + 1, dtype=jnp.int32)
    outs = []
    for g, w in enumerate(POOL_WINDOWS):
        c = cs[:, :, g]
        prev = jnp.pad(c[:, :-w], ((0, 0), (w, 0), (0, 0)))
        cnt = jnp.minimum(pos, w).astype(jnp.float32)[None, :, None]
        outs.append((c - prev) / cnt)
    pooled = jnp.stack(outs, axis=2) - pg.astype(jnp.float32)
    return pooled.astype(p.dtype)


def pooling_branch(p, pool_w, pool_scale, w_out):
    B, S, _ = p.shape
    z = causal_multiscale_pool(p)
    z = jnp.einsum("bsgc,gcd->bsgd", z, pool_w).reshape(B, S, C_POOL)
    z = z * pool_scale
    return z @ w_out


def setup_inputs(seed: int = 0) -> dict:
    key = jax.random.key(seed)
    ks = jax.random.split(key, 20)
    f32 = jnp.float32
    nrm = lambda k, shape, s: jax.random.normal(k, shape, f32) * s
    return {
        "x": jax.random.normal(ks[0], (BATCH, SEQ, D_MODEL), f32),
        "mix_pre_g": 1.0 + nrm(ks[1], (D_MODEL,), 0.05),
        "w_in": nrm(ks[2], (D_MODEL, IN_COLS), D_MODEL ** -0.5),
        "dw_kernel": nrm(ks[3], (CONV_WIDTH, C_CONV), CONV_WIDTH ** -0.5),
        "dw_bias": nrm(ks[4], (C_CONV,), 0.02),
        "conv_ln_g": 1.0 + nrm(ks[5], (C_CONV,), 0.05),
        "conv_ln_b": nrm(ks[6], (C_CONV,), 0.02),
        "w_conv_out": nrm(ks[7], (C_CONV, D_MODEL), C_CONV ** -0.5),
        "pool_w": nrm(ks[8], (N_POOL_GROUPS, POOL_GROUP, POOL_GROUP), POOL_GROUP ** -0.5),
        "pool_scale": 1.0 + nrm(ks[9], (C_POOL,), 0.1),
        "w_pool_out": nrm(ks[10], (C_POOL, D_MODEL), C_POOL ** -0.5),
        "w_o": nrm(ks[11], (D_MODEL, D_MODEL), D_MODEL ** -0.5),
        "mix_post_g": 1.0 + nrm(ks[12], (D_MODEL,), 0.05),
        "mlp_pre_g": 1.0 + nrm(ks[13], (D_MODEL,), 0.05),
        "w_ff1": nrm(ks[14], (D_MODEL, D_FF), D_MODEL ** -0.5),
        "w_ff2": nrm(ks[15], (D_FF, D_MODEL), D_FF ** -0.5),
        "mlp_post_g": 1.0 + nrm(ks[16], (D_MODEL,), 0.05),
    }


def reference(x, mix_pre_g, w_in, dw_kernel, dw_bias, conv_ln_g, conv_ln_b,
              w_conv_out, pool_w, pool_scale, w_pool_out, w_o, mix_post_g,
              mlp_pre_g, w_ff1, w_ff2, mlp_post_g):
    h = x
    for _ in range(DEPTH):
        u = rms_norm(h, mix_pre_g)
        proj = u @ w_in
        u_glu = proj[..., :2 * C_CONV]
        p = proj[..., 2 * C_CONV:2 * C_CONV + C_POOL]
        gates = jax.nn.sigmoid(proj[..., 2 * C_CONV + C_POOL:])
        g_conv, g_pool = jnp.split(gates, N_BRANCHES, axis=-1)
        y_conv = conformer_conv_branch(u_glu, dw_kernel, dw_bias, conv_ln_g, conv_ln_b, w_conv_out)
        y_pool = pooling_branch(p, pool_w, pool_scale, w_pool_out)
        merged = g_conv * y_conv + g_pool * y_pool
        h = h + rms_norm(merged @ w_o, mix_post_g)
        v = rms_norm(h, mlp_pre_g)
        v = jnp.square(jax.nn.relu(v @ w_ff1)) @ w_ff2
        h = h + rms_norm(v, mlp_post_g)
    return h
```

```python
import functools

import jax
import jax.numpy as jnp
from jax import lax
from jax.experimental import pallas as pl
from jax.experimental.pallas import tpu as pltpu

D_MODEL = 1024
CONV_WIDTH = 31
POOL_WINDOWS = (2, 4, 8, 16)
POOL_GROUP = D_MODEL // len(POOL_WINDOWS)
D_FF = 4 * D_MODEL
RMS_EPS = 1e-6
LN_EPS = 1e-5

SUBLANES = 8
CONV_HALO = 32
POOL_HALO = 16
ROW_CHUNK = 16
CONV_COLS = 256
MIX_TILE = 512
MLP_TILE = 512
VMEM_LIMIT_BYTES = 56 * 1024 * 1024

assert CONV_HALO >= CONV_WIDTH - 1 and CONV_HALO % SUBLANES == 0
assert POOL_HALO >= max(POOL_WINDOWS) - 1 and POOL_HALO % SUBLANES == 0

_F32 = jnp.float32
_BF16 = jnp.bfloat16


def _dot(a, b):
    return jnp.dot(a, b, preferred_element_type=_F32)


def _rms_scale(v):
    return lax.rsqrt(jnp.mean(v * v, axis=-1, keepdims=True) + RMS_EPS)


def _row_loop(n_rows, body):
    def step(i, carry):
        body(pl.multiple_of(i * ROW_CHUNK, ROW_CHUNK))
        return carry
    lax.fori_loop(0, n_rows // ROW_CHUNK, step, 0)


def _mixer_kernel(x_ref, g_pre_ref, w_in_ref, dwk_ref, dwb_ref, lng_ref, lnb_ref,
                  wco_ref, pw_ref, psc_ref, wpo_ref, wo_ref, g_post_ref,
                  o_ref,
                  u_scr, glu_scr, p_scr, cu_scr, z_scr, ya_scr, yb_scr, yc_scr, m_scr, sh_scr):
    tm = x_ref.shape[0]
    d = D_MODEL
    seq_tile = pl.program_id(1)
    rows = pl.ds

    @pl.when(seq_tile == 0)
    def _():
        glu_scr[0:CONV_HALO, :] = jnp.zeros((CONV_HALO, d), _F32)
        p_scr[0:POOL_HALO, :] = jnp.zeros((POOL_HALO, d), _F32)

    def prenorm(r0):
        x = x_ref[rows(r0, ROW_CHUNK), :]
        u_scr[rows(r0, ROW_CHUNK), :] = (x * _rms_scale(x) * g_pre_ref[...]).astype(_BF16)
    _row_loop(tm, prenorm)

    ya_scr[...] = _dot(u_scr[...], w_in_ref[:, 0:d])
    yb_scr[...] = _dot(u_scr[...], w_in_ref[:, d:2 * d])
    p_scr[POOL_HALO:POOL_HALO + tm, :] = _dot(u_scr[...], w_in_ref[:, 2 * d:3 * d])

    def glu(r0):
        a = ya_scr[rows(r0, ROW_CHUNK), :]
        gate = yb_scr[rows(r0, ROW_CHUNK), :]
        glu_scr[rows(CONV_HALO + r0, ROW_CHUNK), :] = a * jax.nn.sigmoid(gate)
    _row_loop(tm, glu)

    conv_base = CONV_HALO - (CONV_WIDTH - 1)
    sh_rows = sh_scr.shape[1]
    for cb in range(d // CONV_COLS):
        cols = slice(cb * CONV_COLS, (cb + 1) * CONV_COLS)
        for r in range(1, SUBLANES):
            sh_scr[r - 1] = glu_scr[r:r + sh_rows, cols]

        def conv(r0, cols=cols):
            acc = jnp.zeros((ROW_CHUNK, CONV_COLS), _F32)
            for k in range(CONV_WIDTH):
                q, r = divmod(conv_base + k, SUBLANES)
                start = r0 + q * SUBLANES
                if r == 0:
                    src = glu_scr[rows(start, ROW_CHUNK), cols]
                else:
                    src = sh_scr[r - 1, rows(start, ROW_CHUNK), :]
                acc = acc + dwk_ref[k:k + 1, cols] * src
            ya_scr[rows(r0, ROW_CHUNK), cols] = acc
        _row_loop(tm, conv)

    def conv_norm(r0):
        acc = ya_scr[rows(r0, ROW_CHUNK), :] + dwb_ref[...]
        mu = jnp.mean(acc, axis=-1, keepdims=True)
        cen = acc - mu
        var = jnp.mean(cen * cen, axis=-1, keepdims=True)
        y = cen * lax.rsqrt(var + LN_EPS) * lng_ref[...] + lnb_ref[...]
        cu_scr[rows(r0, ROW_CHUNK), :] = (y * jax.nn.sigmoid(y)).astype(_BF16)
    _row_loop(tm, conv_norm)

    def pool(r0):
        pos = seq_tile * tm + r0 + 1 + lax.broadcasted_iota(jnp.int32, (ROW_CHUNK, 1), 0)
        for g, w in enumerate(POOL_WINDOWS):
            cols = slice(g * POOL_GROUP, (g + 1) * POOL_GROUP)
            win = p_scr[rows(r0, POOL_HALO + ROW_CHUNK), cols]
            cur = win[POOL_HALO:, :]
            acc = win
            step = 1
            while step < w:
                acc = acc[step:, :] + acc[:-step, :]
                step *= 2
            acc = acc[POOL_HALO - (w - 1):, :]
            cnt = jnp.minimum(pos, w).astype(_F32)
            z_scr[rows(r0, ROW_CHUNK), cols] = (acc / cnt - cur).astype(_BF16)
    _row_loop(tm, pool)

    for g in range(len(POOL_WINDOWS)):
        cols = slice(g * POOL_GROUP, (g + 1) * POOL_GROUP)
        ya_scr[:, cols] = _dot(z_scr[:, cols], pw_ref[g])

    def pool_scale(r0):
        z_scr[rows(r0, ROW_CHUNK), :] = (
            ya_scr[rows(r0, ROW_CHUNK), :] * psc_ref[...]).astype(_BF16)
    _row_loop(tm, pool_scale)

    ya_scr[...] = _dot(cu_scr[...], wco_ref[...])
    yb_scr[...] = _dot(u_scr[...], w_in_ref[:, 3 * d:4 * d])

    def merge_conv(r0):
        y = ya_scr[rows(r0, ROW_CHUNK), :]
        gate = yb_scr[rows(r0, ROW_CHUNK), :]
        ya_scr[rows(r0, ROW_CHUNK), :] = jax.nn.sigmoid(gate) * y
    _row_loop(tm, merge_conv)

    yb_scr[...] = _dot(z_scr[...], wpo_ref[...])
    yc_scr[...] = _dot(u_scr[...], w_in_ref[:, 4 * d:5 * d])

    def merge_pool(r0):
        y = yb_scr[rows(r0, ROW_CHUNK), :]
        gate = yc_scr[rows(r0, ROW_CHUNK), :]
        merged = ya_scr[rows(r0, ROW_CHUNK), :] + jax.nn.sigmoid(gate) * y
        m_scr[rows(r0, ROW_CHUNK), :] = merged.astype(_BF16)
    _row_loop(tm, merge_pool)

    ya_scr[...] = _dot(m_scr[...], wo_ref[...])

    def post(r0):
        v = ya_scr[rows(r0, ROW_CHUNK), :]
        o_ref[rows(r0, ROW_CHUNK), :] = (
            x_ref[rows(r0, ROW_CHUNK), :] + v * _rms_scale(v) * g_post_ref[...])
    _row_loop(tm, post)

    glu_scr[0:CONV_HALO, :] = glu_scr[tm:tm + CONV_HALO, :]
    p_scr[0:POOL_HALO, :] = p_scr[tm:tm + POOL_HALO, :]


def _mlp_kernel(h_ref, g_pre_ref, w1_ref, w2_ref, g_post_ref, o_ref,
                v_scr, hid_f32_scr, hid_scr, y_scr):
    tm = h_ref.shape[0]
    d = D_MODEL
    rows = pl.ds

    def prenorm(r0):
        h = h_ref[rows(r0, ROW_CHUNK), :]
        v_scr[rows(r0, ROW_CHUNK), :] = (h * _rms_scale(h) * g_pre_ref[...]).astype(_BF16)
    _row_loop(tm, prenorm)

    for c in range(D_FF // d):
        cols = slice(c * d, (c + 1) * d)
        hid_f32_scr[...] = _dot(v_scr[...], w1_ref[:, cols])

        def relu2(r0, cols=cols):
            a = jnp.maximum(hid_f32_scr[rows(r0, ROW_CHUNK), :], 0.0)
            hid_scr[rows(r0, ROW_CHUNK), cols] = (a * a).astype(_BF16)
        _row_loop(tm, relu2)

    y_scr[...] = _dot(hid_scr[...], w2_ref[...])

    def post(r0):
        v = y_scr[rows(r0, ROW_CHUNK), :]
        o_ref[rows(r0, ROW_CHUNK), :] = (
            h_ref[rows(r0, ROW_CHUNK), :] + v * _rms_scale(v) * g_post_ref[...])
    _row_loop(tm, post)


def _resident(shape):
    zeros = (0,) * len(shape)
    return pl.BlockSpec(shape, lambda *_: zeros, pipeline_mode=pl.Buffered(1))


def _mixer(x, g_pre, w_in, dwk, dwb, lng, lnb, wco, pw, psc, wpo, wo, g_post):
    batch, seq, d = x.shape
    tm = MIX_TILE
    tile = pl.BlockSpec((None, tm, d), lambda b, s: (b, s, 0))
    params = (g_pre, w_in, dwk, dwb, lng, lnb, wco, pw, psc, wpo, wo, g_post)
    return pl.pallas_call(
        _mixer_kernel,
        grid=(batch, seq // tm),
        in_specs=[tile] + [_resident(p.shape) for p in params],
        out_specs=tile,
        out_shape=jax.ShapeDtypeStruct(x.shape, x.dtype),
        scratch_shapes=[
            pltpu.VMEM((tm, d), _BF16),
            pltpu.VMEM((CONV_HALO + tm, d), _F32),
            pltpu.VMEM((POOL_HALO + tm, d), _F32),
            pltpu.VMEM((tm, d), _BF16),
            pltpu.VMEM((tm, d), _BF16),
            pltpu.VMEM((tm, d), _F32),
            pltpu.VMEM((tm, d), _F32),
            pltpu.VMEM((tm, d), _F32),
            pltpu.VMEM((tm, d), _BF16),
            pltpu.VMEM((SUBLANES - 1, tm + CONV_HALO - SUBLANES, CONV_COLS), _F32),
        ],
        compiler_params=pltpu.CompilerParams(
            dimension_semantics=("arbitrary", "arbitrary"),
            vmem_limit_bytes=VMEM_LIMIT_BYTES),
        name="mixer",
    )(x, *params)


def _mlp(h, g_pre, w1, w2, g_post):
    tokens, d = h.shape
    tm = MLP_TILE
    tile = pl.BlockSpec((tm, d), lambda i: (i, 0))
    params = (g_pre, w1, w2, g_post)
    return pl.pallas_call(
        _mlp_kernel,
        grid=(tokens // tm,),
        in_specs=[tile] + [_resident(p.shape) for p in params],
        out_specs=tile,
        out_shape=jax.ShapeDtypeStruct(h.shape, h.dtype),
        scratch_shapes=[
            pltpu.VMEM((tm, d), _BF16),
            pltpu.VMEM((tm, d), _F32),
            pltpu.VMEM((tm, D_FF), _BF16),
            pltpu.VMEM((tm, d), _F32),
        ],
        compiler_params=pltpu.CompilerParams(
            dimension_semantics=("arbitrary",),
            vmem_limit_bytes=VMEM_LIMIT_BYTES),
        name="mlp",
    )(h, *params)


def kernel(x, mix_pre_g, w_in, dw_kernel, dw_bias, conv_ln_g, conv_ln_b, w_conv_out,
           pool_w, pool_scale, w_pool_out, w_o, mix_post_g, mlp_pre_g, w_ff1, w_ff2,
           mlp_post_g):
    batch, seq, d = x.shape
    row = lambda v: v.reshape(1, -1)
    h = _mixer(
        x, row(mix_pre_g), w_in.astype(_BF16), dw_kernel, row(dw_bias),
        row(conv_ln_g), row(conv_ln_b), w_conv_out.astype(_BF16),
        pool_w.astype(_BF16), row(pool_scale), w_pool_out.astype(_BF16),
        w_o.astype(_BF16), row(mix_post_g))
    out = _mlp(h.reshape(batch * seq, d), row(mlp_pre_g), w_ff1.astype(_BF16),
               w_ff2.astype(_BF16), row(mlp_post_g))
    return out.reshape(batch, seq, d)
```

```python
import jax
import jax.numpy as jnp
from jax import lax
from jax.experimental import pallas as pl
from jax.experimental.pallas import tpu as pltpu

D_MODEL = 1024
CONV_WIDTH = 31
POOL_WINDOWS = (2, 4, 8, 16)
POOL_GROUP = D_MODEL // len(POOL_WINDOWS)
D_FF = 4 * D_MODEL
RMS_EPS = 1e-6
LN_EPS = 1e-5

SUBLANES = 8
CONV_HALO = 32
POOL_HALO = 16
ROW_CHUNK = 64
CONV_COLS = 256
MIX_TILE = 512
MLP_TILE = 512
VMEM_LIMIT_BYTES = 56 * 1024 * 1024

assert CONV_HALO >= CONV_WIDTH - 1 and CONV_HALO % SUBLANES == 0
assert POOL_HALO >= max(POOL_WINDOWS) - 1 and POOL_HALO % SUBLANES == 0

_F32 = jnp.float32
_BF16 = jnp.bfloat16


def _dot(a, b):
    return jnp.dot(a, b, preferred_element_type=_F32)


def _rms_scale(v):
    return lax.rsqrt(jnp.mean(v * v, axis=-1, keepdims=True) + RMS_EPS)


def _mixer_kernel(x_ref, g_pre_ref, w_in_ref, dwk_ref, dwb_ref, lng_ref, lnb_ref,
                  wco_ref, pw_ref, psc_ref, wpo_ref, wo_ref, g_post_ref,
                  o_ref,
                  u_scr, glu_scr, p_scr, cu_scr, z_scr, ya_scr, yb_scr, gc_scr, gp_scr,
                  m_scr, sh_scr):
    tm = x_ref.shape[0]
    d = D_MODEL
    seq_tile = pl.program_id(1)
    chunks = [slice(r0, r0 + ROW_CHUNK) for r0 in range(0, tm, ROW_CHUNK)]

    @pl.when(seq_tile == 0)
    def _():
        glu_scr[0:CONV_HALO, :] = jnp.zeros((CONV_HALO, d), _F32)
        p_scr[0:POOL_HALO, :] = jnp.zeros((POOL_HALO, d), _F32)

    x = x_ref[...]
    u_scr[...] = (x * _rms_scale(x) * g_pre_ref[...]).astype(_BF16)

    ya_scr[...] = _dot(u_scr[...], w_in_ref[:, 0:d])
    glu_scr[CONV_HALO:CONV_HALO + tm, :] = (
        ya_scr[...] * jax.nn.sigmoid(_dot(u_scr[...], w_in_ref[:, d:2 * d])))
    p_scr[POOL_HALO:POOL_HALO + tm, :] = _dot(u_scr[...], w_in_ref[:, 2 * d:3 * d])
    gc_scr[...] = jax.nn.sigmoid(_dot(u_scr[...], w_in_ref[:, 3 * d:4 * d]))
    gp_scr[...] = jax.nn.sigmoid(_dot(u_scr[...], w_in_ref[:, 4 * d:5 * d]))

    conv_base = CONV_HALO - (CONV_WIDTH - 1)
    sh_rows = sh_scr.shape[1]
    for cb in range(d // CONV_COLS):
        cols = slice(cb * CONV_COLS, (cb + 1) * CONV_COLS)
        for r in range(1, SUBLANES):
            sh_scr[r - 1] = glu_scr[r:r + sh_rows, cols]
        for rc in chunks:
            acc = None
            for k in range(CONV_WIDTH):
                q, r = divmod(conv_base + k, SUBLANES)
                src_rows = slice(rc.start + q * SUBLANES, rc.stop + q * SUBLANES)
                src = glu_scr[src_rows, cols] if r == 0 else sh_scr[r - 1, src_rows, :]
                term = dwk_ref[k:k + 1, cols] * src
                acc = term if acc is None else acc + term
            ya_scr[rc, cols] = acc

    for rc in chunks:
        acc = ya_scr[rc, :] + dwb_ref[...]
        mu = jnp.mean(acc, axis=-1, keepdims=True)
        cen = acc - mu
        var = jnp.mean(cen * cen, axis=-1, keepdims=True)
        y = cen * lax.rsqrt(var + LN_EPS) * lng_ref[...] + lnb_ref[...]
        cu_scr[rc, :] = (y * jax.nn.sigmoid(y)).astype(_BF16)

    for rc in chunks:
        pos = (seq_tile * tm + rc.start + 1
               + lax.broadcasted_iota(jnp.int32, (ROW_CHUNK, 1), 0))
        for g, w in enumerate(POOL_WINDOWS):
            cols = slice(g * POOL_GROUP, (g + 1) * POOL_GROUP)
            win = p_scr[rc.start:rc.stop + POOL_HALO, cols]
            cur = win[POOL_HALO:, :]
            acc = win
            step = 1
            while step < w:
                acc = acc[step:, :] + acc[:-step, :]
                step *= 2
            acc = acc[POOL_HALO - (w - 1):, :]
            inv_cnt = 1.0 / jnp.minimum(pos, w).astype(_F32)
            z_scr[rc, cols] = (acc * inv_cnt - cur).astype(_BF16)

    for g in range(len(POOL_WINDOWS)):
        cols = slice(g * POOL_GROUP, (g + 1) * POOL_GROUP)
        m_scr[:, cols] = (_dot(z_scr[:, cols], pw_ref[g]) * psc_ref[:, cols]).astype(_BF16)
    yb_scr[...] = gp_scr[...] * _dot(m_scr[...], wpo_ref[...])

    m_scr[...] = (gc_scr[...] * _dot(cu_scr[...], wco_ref[...]) + yb_scr[...]).astype(_BF16)

    ya_scr[...] = _dot(m_scr[...], wo_ref[...])
    y = ya_scr[...]
    o_ref[...] = x_ref[...] + y * _rms_scale(y) * g_post_ref[...]

    glu_scr[0:CONV_HALO, :] = glu_scr[tm:tm + CONV_HALO, :]
    p_scr[0:POOL_HALO, :] = p_scr[tm:tm + POOL_HALO, :]


def _mlp_kernel(h_ref, g_pre_ref, w1_ref, w2_ref, g_post_ref, o_ref,
                v_scr, hid_scr, y_scr):
    d = D_MODEL
    h = h_ref[...]
    v_scr[...] = (h * _rms_scale(h) * g_pre_ref[...]).astype(_BF16)

    for c in range(D_FF // d):
        cols = slice(c * d, (c + 1) * d)
        a = jnp.maximum(_dot(v_scr[...], w1_ref[:, cols]), 0.0)
        hid_scr[:, cols] = (a * a).astype(_BF16)

    y_scr[...] = _dot(hid_scr[...], w2_ref[...])
    y = y_scr[...]
    o_ref[...] = h_ref[...] + y * _rms_scale(y) * g_post_ref[...]


def _resident(shape):
    zeros = (0,) * len(shape)
    return pl.BlockSpec(shape, lambda *_: zeros, pipeline_mode=pl.Buffered(1))


def _mixer(x, g_pre, w_in, dwk, dwb, lng, lnb, wco, pw, psc, wpo, wo, g_post):
    batch, seq, d = x.shape
    tm = MIX_TILE
    tile = pl.BlockSpec((None, tm, d), lambda b, s: (b, s, 0))
    params = (g_pre, w_in, dwk, dwb, lng, lnb, wco, pw, psc, wpo, wo, g_post)
    return pl.pallas_call(
        _mixer_kernel,
        grid=(batch, seq // tm),
        in_specs=[tile] + [_resident(p.shape) for p in params],
        out_specs=tile,
        out_shape=jax.ShapeDtypeStruct(x.shape, x.dtype),
        scratch_shapes=[
            pltpu.VMEM((tm, d), _BF16),
            pltpu.VMEM((CONV_HALO + tm, d), _F32),
            pltpu.VMEM((POOL_HALO + tm, d), _F32),
            pltpu.VMEM((tm, d), _BF16),
            pltpu.VMEM((tm, d), _BF16),
            pltpu.VMEM((tm, d), _F32),
            pltpu.VMEM((tm, d), _F32),
            pltpu.VMEM((tm, d), _F32),
            pltpu.VMEM((tm, d), _F32),
            pltpu.VMEM((tm, d), _BF16),
            pltpu.VMEM((SUBLANES - 1, tm + CONV_HALO - SUBLANES, CONV_COLS), _F32),
        ],
        compiler_params=pltpu.CompilerParams(
            dimension_semantics=("arbitrary", "arbitrary"),
            vmem_limit_bytes=VMEM_LIMIT_BYTES),
        name="mixer",
    )(x, *params)


def _mlp(h, g_pre, w1, w2, g_post):
    tokens, d = h.shape
    tm = MLP_TILE
    tile = pl.BlockSpec((tm, d), lambda i: (i, 0))
    params = (g_pre, w1, w2, g_post)
    return pl.pallas_call(
        _mlp_kernel,
        grid=(tokens // tm,),
        in_specs=[tile] + [_resident(p.shape) for p in params],
        out_specs=tile,
        out_shape=jax.ShapeDtypeStruct(h.shape, h.dtype),
        scratch_shapes=[
            pltpu.VMEM((tm, d), _BF16),
            pltpu.VMEM((tm, D_FF), _BF16),
            pltpu.VMEM((tm, d), _F32),
        ],
        compiler_params=pltpu.CompilerParams(
            dimension_semantics=("arbitrary",),
            vmem_limit_bytes=VMEM_LIMIT_BYTES),
        name="mlp",
    )(h, *params)


def kernel(x, mix_pre_g, w_in, dw_kernel, dw_bias, conv_ln_g, conv_ln_b, w_conv_out,
           pool_w, pool_scale, w_pool_out, w_o, mix_post_g, mlp_pre_g, w_ff1, w_ff2,
           mlp_post_g):
    batch, seq, d = x.shape
    row = lambda v: v.reshape(1, -1)
    h = _mixer(
        x, row(mix_pre_g), w_in.astype(_BF16), dw_kernel, row(dw_bias),
        row(conv_ln_g), row(conv_ln_b), w_conv_out.astype(_BF16),
        pool_w.astype(_BF16), row(pool_scale), w_pool_out.astype(_BF16),
        w_o.astype(_BF16), row(mix_post_g))
    out = _mlp(h.reshape(batch * seq, d), row(mlp_pre_g), w_ff1.astype(_BF16),
               w_ff2.astype(_BF16), row(mlp_post_g))
    return out.reshape(batch, seq, d)
```

```python
import jax
import jax.numpy as jnp
from jax import lax
from jax.experimental import pallas as pl
from jax.experimental.pallas import tpu as pltpu

D_MODEL = 1024
CONV_WIDTH = 31
POOL_WINDOWS = (2, 4, 8, 16)
POOL_GROUP = D_MODEL // len(POOL_WINDOWS)
D_FF = 4 * D_MODEL
RMS_EPS = 1e-6
LN_EPS = 1e-5

SUBLANES = 8
CONV_HALO = 32
POOL_HALO = 16
ROW_CHUNK = 64
CONV_COLS = 256
CHUNKS_PER_PACE = 3
SHIFT_BUFFERS = 2
MIX_TILE = 512
MLP_TILE = 1024
VMEM_LIMIT_BYTES = 56 * 1024 * 1024

assert CONV_HALO >= CONV_WIDTH - 1 and CONV_HALO % SUBLANES == 0
assert POOL_HALO >= max(POOL_WINDOWS) - 1 and POOL_HALO % SUBLANES == 0

_F32 = jnp.float32
_BF16 = jnp.bfloat16


def _dot(a, b):
    return jnp.dot(a, b, preferred_element_type=_F32)


def _rms_scale(v):
    return lax.rsqrt(jnp.mean(v * v, axis=-1, keepdims=True) + RMS_EPS)


def _zero_bits(v):
    bits = lax.bitcast_convert_type(v, jnp.uint32)
    half = jnp.uint32(16)
    return lax.shift_right_logical(lax.shift_right_logical(bits, half), half)


def _mixer_kernel(x_ref, g_pre_ref, w_in_ref, dwk_ref, dwb_ref, lng_ref, lnb_ref,
                  wco_ref, pw_ref, psc_ref, wpo_ref, wo_ref, g_post_ref,
                  o_ref,
                  u_scr, p_scr, cu_scr, z_scr, ya_scr, yb_scr, gc_scr, gp_scr,
                  m_scr, wb_scr, *block_scr):
    tm = x_ref.shape[0]
    d = D_MODEL
    seq_tile = pl.program_id(1)
    chunks = [slice(r0, r0 + ROW_CHUNK) for r0 in range(0, tm, ROW_CHUNK)]
    n_blocks = d // CONV_COLS
    glu_blocks = block_scr[:n_blocks]
    sh_bufs = block_scr[n_blocks:]

    @pl.when(seq_tile == 0)
    def _():
        for glu in glu_blocks:
            glu[0:CONV_HALO, :] = jnp.zeros((CONV_HALO, CONV_COLS), _F32)
        p_scr[0:POOL_HALO, :] = jnp.zeros((POOL_HALO, d), _F32)

    @pl.when((pl.program_id(0) == 0) & (seq_tile == 0))
    def _():
        for k in range(CONV_WIDTH):
            wb_scr[k] = jnp.broadcast_to(dwk_ref[k:k + 1, :], (SUBLANES, d))

    x = x_ref[...]
    u_scr[...] = (x * _rms_scale(x) * g_pre_ref[...]).astype(_BF16)

    conv_base = CONV_HALO - (CONV_WIDTH - 1)
    sh_rows = sh_bufs[0].shape[1]
    groups = ROW_CHUNK // SUBLANES

    pace = []
    for cb in range(n_blocks):
        cols = slice(cb * CONV_COLS, (cb + 1) * CONV_COLS)
        gate_cols = slice(d + cols.start, d + cols.stop)
        a = _dot(u_scr[...], w_in_ref[:, cols])
        gate = _dot(u_scr[...], w_in_ref[:, gate_cols])
        glu_blocks[cb][CONV_HALO:CONV_HALO + tm, :] = a * jax.nn.sigmoid(gate)
        pace.append((gate, 0))
    p = _dot(u_scr[...], w_in_ref[:, 2 * d:3 * d])
    p_scr[POOL_HALO:POOL_HALO + tm, :] = p
    gc = _dot(u_scr[...], w_in_ref[:, 3 * d:4 * d])
    gc_scr[...] = jax.nn.sigmoid(gc)
    gp = _dot(u_scr[...], w_in_ref[:, 4 * d:5 * d])
    gp_scr[...] = jax.nn.sigmoid(gp)
    for res in (p, gc, gp):
        pace += [(res, 0), (res, d // 2)]

    def pace_zeros(i):
        res, col = pace[min(i // CHUNKS_PER_PACE, len(pace) - 1)]
        return _zero_bits(res[0:SUBLANES, col:col + CONV_COLS])

    for cb in range(n_blocks):
        cols = slice(cb * CONV_COLS, (cb + 1) * CONV_COLS)
        glu = glu_blocks[cb]
        sh = sh_bufs[cb % len(sh_bufs)]
        for r in range(1, SUBLANES):
            sh[r - 1] = glu[r:r + sh_rows, :]
        for ci, rc in enumerate(chunks):
            zeros = pace_zeros(cb * len(chunks) + ci)
            acc = None
            for k in range(CONV_WIDTH):
                q, r = divmod(conv_base + k, SUBLANES)
                src_rows = slice(rc.start + q * SUBLANES, rc.stop + q * SUBLANES)
                src = glu[src_rows, :] if r == 0 else sh[r - 1, src_rows, :]
                w = lax.bitcast_convert_type(
                    lax.bitcast_convert_type(wb_scr[k, :, cols], jnp.uint32) | zeros, _F32)
                term = w * src.reshape(groups, SUBLANES, CONV_COLS)
                acc = term if acc is None else acc + term
            ya_scr[rc, cols] = acc.reshape(ROW_CHUNK, CONV_COLS)

    for rc in chunks:
        acc = ya_scr[rc, :] + dwb_ref[...]
        mu = jnp.mean(acc, axis=-1, keepdims=True)
        cen = acc - mu
        var = jnp.mean(cen * cen, axis=-1, keepdims=True)
        y = cen * lax.rsqrt(var + LN_EPS) * lng_ref[...] + lnb_ref[...]
        cu_scr[rc, :] = (y * jax.nn.sigmoid(y)).astype(_BF16)

    for rc in chunks:
        pos = (seq_tile * tm + rc.start + 1
               + lax.broadcasted_iota(jnp.int32, (ROW_CHUNK, 1), 0))
        for g, w in enumerate(POOL_WINDOWS):
            cols = slice(g * POOL_GROUP, (g + 1) * POOL_GROUP)
            win = p_scr[rc.start:rc.stop + POOL_HALO, cols]
            cur = win[POOL_HALO:, :]
            acc = win
            step = 1
            while step < w:
                acc = acc[step:, :] + acc[:-step, :]
                step *= 2
            acc = acc[POOL_HALO - (w - 1):, :]
            inv_cnt = 1.0 / jnp.minimum(pos, w).astype(_F32)
            z_scr[rc, cols] = (acc * inv_cnt - cur).astype(_BF16)

    for g in range(len(POOL_WINDOWS)):
        cols = slice(g * POOL_GROUP, (g + 1) * POOL_GROUP)
        m_scr[:, cols] = (_dot(z_scr[:, cols], pw_ref[g]) * psc_ref[:, cols]).astype(_BF16)
    yb_scr[...] = gp_scr[...] * _dot(m_scr[...], wpo_ref[...])

    m_scr[...] = (gc_scr[...] * _dot(cu_scr[...], wco_ref[...]) + yb_scr[...]).astype(_BF16)

    ya_scr[...] = _dot(m_scr[...], wo_ref[...])
    y = ya_scr[...]
    o_ref[...] = x_ref[...] + y * _rms_scale(y) * g_post_ref[...]

    for glu in glu_blocks:
        glu[0:CONV_HALO, :] = glu[tm:tm + CONV_HALO, :]
    p_scr[0:POOL_HALO, :] = p_scr[tm:tm + POOL_HALO, :]


def _mlp_kernel(h_ref, g_pre_ref, w1_ref, w2_ref, g_post_ref, o_ref,
                v_scr, hid_scr, y_scr):
    d = D_MODEL
    h = h_ref[...]
    v_scr[...] = (h * _rms_scale(h) * g_pre_ref[...]).astype(_BF16)

    for c in range(D_FF // d):
        cols = slice(c * d, (c + 1) * d)
        a = jnp.maximum(_dot(v_scr[...], w1_ref[:, cols]), 0.0)
        hid_scr[:, cols] = (a * a).astype(_BF16)

    y_scr[...] = _dot(hid_scr[...], w2_ref[...])
    y = y_scr[...]
    o_ref[...] = h_ref[...] + y * _rms_scale(y) * g_post_ref[...]


def _resident(shape):
    zeros = (0,) * len(shape)
    return pl.BlockSpec(shape, lambda *_: zeros, pipeline_mode=pl.Buffered(1))


def _mixer(x, g_pre, w_in, dwk, dwb, lng, lnb, wco, pw, psc, wpo, wo, g_post):
    batch, seq, d = x.shape
    tm = MIX_TILE
    tile = pl.BlockSpec((None, tm, d), lambda b, s: (b, s, 0))
    params = (g_pre, w_in, dwk, dwb, lng, lnb, wco, pw, psc, wpo, wo, g_post)
    return pl.pallas_call(
        _mixer_kernel,
        grid=(batch, seq // tm),
        in_specs=[tile] + [_resident(p.shape) for p in params],
        out_specs=tile,
        out_shape=jax.ShapeDtypeStruct(x.shape, x.dtype),
        scratch_shapes=[
            pltpu.VMEM((tm, d), _BF16),
            pltpu.VMEM((POOL_HALO + tm, d), _F32),
            pltpu.VMEM((tm, d), _BF16),
            pltpu.VMEM((tm, d), _BF16),
            pltpu.VMEM((tm, d), _F32),
            pltpu.VMEM((tm, d), _F32),
            pltpu.VMEM((tm, d), _F32),
            pltpu.VMEM((tm, d), _F32),
            pltpu.VMEM((tm, d), _BF16),
            pltpu.VMEM((CONV_WIDTH, SUBLANES, d), _F32),
        ] + [pltpu.VMEM((CONV_HALO + tm, CONV_COLS), _F32)] * (d // CONV_COLS) + [
            pltpu.VMEM((SUBLANES - 1, tm + CONV_HALO - SUBLANES, CONV_COLS), _F32)
        ] * SHIFT_BUFFERS,
        compiler_params=pltpu.CompilerParams(
            dimension_semantics=("arbitrary", "arbitrary"),
            vmem_limit_bytes=VMEM_LIMIT_BYTES),
        name="mixer",
    )(x, *params)


def _mlp(h, g_pre, w1, w2, g_post):
    tokens, d = h.shape
    tm = MLP_TILE
    tile = pl.BlockSpec((tm, d), lambda i: (i, 0))
    params = (g_pre, w1, w2, g_post)
    return pl.pallas_call(
        _mlp_kernel,
        grid=(tokens // tm,),
        in_specs=[tile] + [_resident(p.shape) for p in params],
        out_specs=tile,
        out_shape=jax.ShapeDtypeStruct(h.shape, h.dtype),
        scratch_shapes=[
            pltpu.VMEM((tm, d), _BF16),
            pltpu.VMEM((tm, D_FF), _BF16),
            pltpu.VMEM((tm, d), _F32),
        ],
        compiler_params=pltpu.CompilerParams(
            dimension_semantics=("arbitrary",),
            vmem_limit_bytes=VMEM_LIMIT_BYTES),
        name="mlp",
    )(h, *params)


def kernel(x, mix_pre_g, w_in, dw_kernel, dw_bias, conv_ln_g, conv_ln_b, w_conv_out,
           pool_w, pool_scale, w_pool_out, w_o, mix_post_g, mlp_pre_g, w_ff1, w_ff2,
           mlp_post_g):
    batch, seq, d = x.shape
    row = lambda v: v.reshape(1, -1)
    h = _mixer(
        x, row(mix_pre_g), w_in.astype(_BF16), dw_kernel, row(dw_bias),
        row(conv_ln_g), row(conv_ln_b), w_conv_out.astype(_BF16),
        pool_w.astype(_BF16), row(pool_scale), w_pool_out.astype(_BF16),
        w_o.astype(_BF16), row(mix_post_g))
    out = _mlp(h.reshape(batch * seq, d), row(mlp_pre_g), w_ff1.astype(_BF16),
               w_ff2.astype(_BF16), row(mlp_post_g))
    return out.reshape(batch, seq, d)
```

```python
import jax
import jax.numpy as jnp
from jax import lax
from jax.experimental import pallas as pl
from jax.experimental.pallas import tpu as pltpu

D_MODEL = 1024
CONV_WIDTH = 31
POOL_WINDOWS = (2, 4, 8, 16)
POOL_GROUP = D_MODEL // len(POOL_WINDOWS)
D_FF = 4 * D_MODEL
RMS_EPS = 1e-6
LN_EPS = 1e-5

SUBLANES = 8
CONV_HALO = 32
POOL_HALO = 16
ROW_CHUNK = 64
CONV_COLS = 256
CHUNKS_PER_PACE = 3
SHIFT_BUFFERS = 2
MIX_TILE = 512
MLP_TILE = 1024
VMEM_LIMIT_BYTES = 56 * 1024 * 1024

assert CONV_HALO >= CONV_WIDTH - 1 and CONV_HALO % SUBLANES == 0
assert POOL_HALO >= max(POOL_WINDOWS) - 1 and POOL_HALO % SUBLANES == 0

_F32 = jnp.float32
_BF16 = jnp.bfloat16


def _dot(a, b):
    return jnp.dot(a, b, preferred_element_type=_F32)


def _rms_scale(v):
    return lax.rsqrt(jnp.mean(v * v, axis=-1, keepdims=True) + RMS_EPS)


def _zero_bits(v):
    bits = lax.bitcast_convert_type(v, jnp.uint32)
    half = jnp.uint32(16)
    return lax.shift_right_logical(lax.shift_right_logical(bits, half), half)


def _mixer_kernel(x_ref, g_pre_ref, w_in_ref, dwk_ref, dwb_ref, lng_ref, lnb_ref,
                  wco_f32_ref, pw_f32_ref, psc_ref, wpo_f32_ref, wo_f32_ref, g_post_ref,
                  w1_slab_ref, w2_slab_ref,
                  o_ref, w1_bf16_ref, w2_bf16_ref,
                  u_scr, p_scr, cu_scr, z_scr, ya_scr, yb_scr, gc_scr, gp_scr,
                  m_scr, wb_scr, wco_ref, pw_ref, wpo_ref, wo_ref, *block_scr):
    w1_bf16_ref[...] = w1_slab_ref[...].astype(_BF16)
    w2_bf16_ref[...] = w2_slab_ref[...].astype(_BF16)

    tm = x_ref.shape[0]
    d = D_MODEL
    seq_tile = pl.program_id(1)
    chunks = [slice(r0, r0 + ROW_CHUNK) for r0 in range(0, tm, ROW_CHUNK)]
    n_blocks = d // CONV_COLS
    glu_blocks = block_scr[:n_blocks]
    sh_bufs = block_scr[n_blocks:]

    @pl.when(seq_tile == 0)
    def _():
        for glu in glu_blocks:
            glu[0:CONV_HALO, :] = jnp.zeros((CONV_HALO, CONV_COLS), _F32)
        p_scr[0:POOL_HALO, :] = jnp.zeros((POOL_HALO, d), _F32)

    @pl.when((pl.program_id(0) == 0) & (seq_tile == 0))
    def _():
        for k in range(CONV_WIDTH):
            wb_scr[k] = jnp.broadcast_to(dwk_ref[k:k + 1, :], (SUBLANES, d))
        for src, dst in ((wco_f32_ref, wco_ref), (wpo_f32_ref, wpo_ref), (wo_f32_ref, wo_ref)):
            for r0 in range(0, d, CONV_COLS):
                dst[r0:r0 + CONV_COLS, :] = src[r0:r0 + CONV_COLS, :].astype(_BF16)
        for g in range(len(POOL_WINDOWS)):
            pw_ref[g] = pw_f32_ref[g].astype(_BF16)

    x = x_ref[...]
    u_scr[...] = (x * _rms_scale(x) * g_pre_ref[...]).astype(_BF16)

    conv_base = CONV_HALO - (CONV_WIDTH - 1)
    sh_rows = sh_bufs[0].shape[1]
    groups = ROW_CHUNK // SUBLANES

    pace = []
    for cb in range(n_blocks):
        cols = slice(cb * CONV_COLS, (cb + 1) * CONV_COLS)
        gate_cols = slice(d + cols.start, d + cols.stop)
        a = _dot(u_scr[...], w_in_ref[:, cols])
        gate = _dot(u_scr[...], w_in_ref[:, gate_cols])
        glu_blocks[cb][CONV_HALO:CONV_HALO + tm, :] = a * jax.nn.sigmoid(gate)
        pace.append((gate, 0))
    p = _dot(u_scr[...], w_in_ref[:, 2 * d:3 * d])
    p_scr[POOL_HALO:POOL_HALO + tm, :] = p
    gc = _dot(u_scr[...], w_in_ref[:, 3 * d:4 * d])
    gc_scr[...] = jax.nn.sigmoid(gc)
    gp = _dot(u_scr[...], w_in_ref[:, 4 * d:5 * d])
    gp_scr[...] = jax.nn.sigmoid(gp)
    for res in (p, gc, gp):
        pace += [(res, 0), (res, d // 2)]

    def pace_zeros(i):
        res, col = pace[min(i // CHUNKS_PER_PACE, len(pace) - 1)]
        return _zero_bits(res[0:SUBLANES, col:col + CONV_COLS])

    for cb in range(n_blocks):
        cols = slice(cb * CONV_COLS, (cb + 1) * CONV_COLS)
        glu = glu_blocks[cb]
        sh = sh_bufs[cb % len(sh_bufs)]
        for r in range(1, SUBLANES):
            sh[r - 1] = glu[r:r + sh_rows, :]
        for ci, rc in enumerate(chunks):
            zeros = pace_zeros(cb * len(chunks) + ci)
            acc = None
            for k in range(CONV_WIDTH):
                q, r = divmod(conv_base + k, SUBLANES)
                src_rows = slice(rc.start + q * SUBLANES, rc.stop + q * SUBLANES)
                src = glu[src_rows, :] if r == 0 else sh[r - 1, src_rows, :]
                w = lax.bitcast_convert_type(
                    lax.bitcast_convert_type(wb_scr[k, :, cols], jnp.uint32) | zeros, _F32)
                term = w * src.reshape(groups, SUBLANES, CONV_COLS)
                acc = term if acc is None else acc + term
            ya_scr[rc, cols] = acc.reshape(ROW_CHUNK, CONV_COLS)

    for rc in chunks:
        acc = ya_scr[rc, :] + dwb_ref[...]
        mu = jnp.mean(acc, axis=-1, keepdims=True)
        cen = acc - mu
        var = jnp.mean(cen * cen, axis=-1, keepdims=True)
        y = cen * lax.rsqrt(var + LN_EPS) * lng_ref[...] + lnb_ref[...]
        cu_scr[rc, :] = (y * jax.nn.sigmoid(y)).astype(_BF16)

    for rc in chunks:
        pos = (seq_tile * tm + rc.start + 1
               + lax.broadcasted_iota(jnp.int32, (ROW_CHUNK, 1), 0))
        for g, w in enumerate(POOL_WINDOWS):
            cols = slice(g * POOL_GROUP, (g + 1) * POOL_GROUP)
            win = p_scr[rc.start:rc.stop + POOL_HALO, cols]
            cur = win[POOL_HALO:, :]
            acc = win
            step = 1
            while step < w:
                acc = acc[step:, :] + acc[:-step, :]
                step *= 2
            acc = acc[POOL_HALO - (w - 1):, :]
            inv_cnt = 1.0 / jnp.minimum(pos, w).astype(_F32)
            z_scr[rc, cols] = (acc * inv_cnt - cur).astype(_BF16)

    for g in range(len(POOL_WINDOWS)):
        cols = slice(g * POOL_GROUP, (g + 1) * POOL_GROUP)
        m_scr[:, cols] = (_dot(z_scr[:, cols], pw_ref[g]) * psc_ref[:, cols]).astype(_BF16)
    yb_scr[...] = gp_scr[...] * _dot(m_scr[...], wpo_ref[...])

    m_scr[...] = (gc_scr[...] * _dot(cu_scr[...], wco_ref[...]) + yb_scr[...]).astype(_BF16)

    ya_scr[...] = _dot(m_scr[...], wo_ref[...])
    y = ya_scr[...]
    o_ref[...] = x_ref[...] + y * _rms_scale(y) * g_post_ref[...]

    for glu in glu_blocks:
        glu[0:CONV_HALO, :] = glu[tm:tm + CONV_HALO, :]
    p_scr[0:POOL_HALO, :] = p_scr[tm:tm + POOL_HALO, :]


def _mlp_kernel(h_ref, g_pre_ref, w1_ref, w2_ref, g_post_ref, o_ref,
                v_scr, hid_scr, y_scr):
    d = D_MODEL
    h = h_ref[...]
    v_scr[...] = (h * _rms_scale(h) * g_pre_ref[...]).astype(_BF16)

    for c in range(D_FF // d):
        cols = slice(c * d, (c + 1) * d)
        a = jnp.maximum(_dot(v_scr[...], w1_ref[:, cols]), 0.0)
        hid_scr[:, cols] = (a * a).astype(_BF16)

    y_scr[...] = _dot(hid_scr[...], w2_ref[...])
    y = y_scr[...]
    o_ref[...] = h_ref[...] + y * _rms_scale(y) * g_post_ref[...]


def _resident(shape):
    zeros = (0,) * len(shape)
    return pl.BlockSpec(shape, lambda *_: zeros, pipeline_mode=pl.Buffered(1))


def _mixer(x, g_pre, w_in, dwk, dwb, lng, lnb, wco, pw, psc, wpo, wo, g_post, w1, w2):
    batch, seq, d = x.shape
    tm = MIX_TILE
    seq_tiles = seq // tm
    steps = batch * seq_tiles
    tile = pl.BlockSpec((None, tm, d), lambda b, s: (b, s, 0))
    params = (g_pre, w_in, dwk, dwb, lng, lnb, wco, pw, psc, wpo, wo, g_post)
    w1_slab = pl.BlockSpec((d, D_FF // steps), lambda b, s: (0, b * seq_tiles + s))
    w2_slab = pl.BlockSpec((D_FF // steps, d), lambda b, s: (b * seq_tiles + s, 0))
    return pl.pallas_call(
        _mixer_kernel,
        grid=(batch, seq_tiles),
        in_specs=[tile] + [_resident(p.shape) for p in params] + [w1_slab, w2_slab],
        out_specs=[tile, w1_slab, w2_slab],
        out_shape=[jax.ShapeDtypeStruct(x.shape, x.dtype),
                   jax.ShapeDtypeStruct(w1.shape, _BF16),
                   jax.ShapeDtypeStruct(w2.shape, _BF16)],
        scratch_shapes=[
            pltpu.VMEM((tm, d), _BF16),
            pltpu.VMEM((POOL_HALO + tm, d), _F32),
            pltpu.VMEM((tm, d), _BF16),
            pltpu.VMEM((tm, d), _BF16),
            pltpu.VMEM((tm, d), _F32),
            pltpu.VMEM((tm, d), _F32),
            pltpu.VMEM((tm, d), _F32),
            pltpu.VMEM((tm, d), _F32),
            pltpu.VMEM((tm, d), _BF16),
            pltpu.VMEM((CONV_WIDTH, SUBLANES, d), _F32),
            pltpu.VMEM(wco.shape, _BF16),
            pltpu.VMEM(pw.shape, _BF16),
            pltpu.VMEM(wpo.shape, _BF16),
            pltpu.VMEM(wo.shape, _BF16),
        ] +[pltpu.VMEM((CONV_HALO + tm, CONV_COLS), _F32)] * (d // CONV_COLS) + [
            pltpu.VMEM((SUBLANES - 1, tm + CONV_HALO - SUBLANES, CONV_COLS), _F32)
        ] * SHIFT_BUFFERS,
        compiler_params=pltpu.CompilerParams(
            dimension_semantics=("arbitrary", "arbitrary"),
            vmem_limit_bytes=VMEM_LIMIT_BYTES),
        name="mixer",
    )(x, *params, w1, w2)


def _mlp(h, g_pre, w1, w2, g_post):
    tokens, d = h.shape
    tm = MLP_TILE
    tile = pl.BlockSpec((tm, d), lambda i: (i, 0))
    params = (g_pre, w1, w2, g_post)
    return pl.pallas_call(
        _mlp_kernel,
        grid=(tokens // tm,),
        in_specs=[tile] + [_resident(p.shape) for p in params],
        out_specs=tile,
        out_shape=jax.ShapeDtypeStruct(h.shape, h.dtype),
        scratch_shapes=[
            pltpu.VMEM((tm, d), _BF16),
            pltpu.VMEM((tm, D_FF), _BF16),
            pltpu.VMEM((tm, d), _F32),
        ],
        compiler_params=pltpu.CompilerParams(
            dimension_semantics=("arbitrary",),
            vmem_limit_bytes=VMEM_LIMIT_BYTES),
        name="mlp",
    )(h, *params)


def kernel(x, mix_pre_g, w_in, dw_kernel, dw_bias, conv_ln_g, conv_ln_b, w_conv_out,
           pool_w, pool_scale, w_pool_out, w_o, mix_post_g, mlp_pre_g, w_ff1, w_ff2,
           mlp_post_g):
    batch, seq, d = x.shape
    row = lambda v: v.reshape(1, -1)
    h, w1_bf16, w2_bf16 = _mixer(
        x, row(mix_pre_g), w_in.astype(_BF16), dw_kernel, row(dw_bias),
        row(conv_ln_g), row(conv_ln_b), w_conv_out, pool_w, row(pool_scale), w_pool_out,
        w_o, row(mix_post_g), w_ff1, w_ff2)
    out = _mlp(h.reshape(batch * seq, d), row(mlp_pre_g), w1_bf16, w2_bf16,
               row(mlp_post_g))
    return out.reshape(batch, seq, d)
```
